```python
import jax, jax.numpy as jnp
from jax import lax
import numpy as np

D_MODEL = 1024
BATCH = 2
SEQ = 8192
DEPTH = 1

PLE_DIM = 256
CONV_HEADS = 8
CONV_HEAD_DIM = 64
CONV_WIDTH = CONV_HEADS * CONV_HEAD_DIM
CONV_K = 3
MLA_HEADS = 8
Q_LORA = 256
KV_LORA = 128
QK_NOPE = 64
QK_ROPE = 32
V_HEAD = 64
MLA_WIDTH = MLA_HEADS * V_HEAD
D_MIX = CONV_WIDTH + MLA_WIDTH
D_IN = 3 * CONV_WIDTH + Q_LORA + KV_LORA + QK_ROPE
D_FF = 2816
FFN_CONV_K = 3
Q_BLOCK = 128
ROPE_THETA = 10000.0
EPS = 1e-6

kernel_name = "hybrid_shortconv_mla_convffn_ple"


def rmsnorm(x, g):
    xf = x.astype(jnp.float32)
    y = xf * lax.rsqrt(jnp.mean(xf * xf, axis=-1, keepdims=True) + EPS)
    return (y * g.astype(jnp.float32)).astype(x.dtype)


def dwconv_centred(x, w, b=None):
    s = x.shape[1]
    xp = jnp.pad(x, ((0, 0), (1, 1), (0, 0)))
    y = xp[:, 0:s] * w[0] + xp[:, 1:s + 1] * w[1] + xp[:, 2:s + 2] * w[2]
    if b is not None:
        y = y + b
    return y


def rope_tables(s, dim, dtype):
    pos = jnp.arange(s, dtype=jnp.float32)
    inv_freq = ROPE_THETA ** (-jnp.arange(0, dim, 2, dtype=jnp.float32) / dim)
    ang = pos[:, None] * inv_freq[None, :]
    ang = jnp.concatenate([ang, ang], axis=-1)
    return jnp.cos(ang).astype(dtype), jnp.sin(ang).astype(dtype)


def rotate_half(x):
    x1, x2 = jnp.split(x, 2, axis=-1)
    return jnp.concatenate([-x2, x1], axis=-1)


def mla_attention(q_nope, q_rope, k_nope, k_rope, v):
    b, s, h, _ = q_nope.shape
    nb = s // Q_BLOCK
    scale = (QK_NOPE + QK_ROPE) ** -0.5
    qn = q_nope.reshape(b, nb, Q_BLOCK, h, QK_NOPE).transpose(1, 0, 2, 3, 4)
    qr = q_rope.reshape(b, nb, Q_BLOCK, h, QK_ROPE).transpose(1, 0, 2, 3, 4)

    def block(args):
        qn_b, qr_b = args
        sc = (jnp.einsum('bqhd,bkhd->bhqk', qn_b, k_nope)
              + jnp.einsum('bqhr,bkr->bhqk', qr_b, k_rope))
        probs = jax.nn.softmax(sc.astype(jnp.float32) * scale, axis=-1).astype(v.dtype)
        return jnp.einsum('bhqk,bkhd->bqhd', probs, v)

    out = lax.map(block, (qn, qr))
    return out.transpose(1, 0, 2, 3, 4).reshape(b, s, h * V_HEAD)


def setup_inputs(seed: int = 0) -> dict:
    key = jax.random.key(seed)
    ks = jax.random.split(key, 20)
    f32 = jnp.float32

    def nrm(k, shape, fan_in):
        return jax.random.normal(k, shape, f32) * (fan_in ** -0.5)

    def gain(k, dim):
        return 1.0 + 0.01 * jax.random.normal(k, (DEPTH, dim), f32)

    return {
        "x": jax.random.normal(ks[0], (BATCH, SEQ, D_MODEL), f32),
        "p": jax.random.normal(ks[1], (DEPTH, BATCH, SEQ, PLE_DIM), f32),
        "norm_mix_g": gain(ks[2], D_MODEL),
        "w_in": nrm(ks[3], (DEPTH, D_MODEL, D_IN), D_MODEL),
        "conv_w": nrm(ks[4], (DEPTH, CONV_K, CONV_WIDTH), CONV_K),
        "q_norm_g": gain(ks[5], Q_LORA),
        "w_uq": nrm(ks[6], (DEPTH, Q_LORA, MLA_HEADS * (QK_NOPE + QK_ROPE)), Q_LORA),
        "kv_norm_g": gain(ks[7], KV_LORA),
        "w_ukv": nrm(ks[8], (DEPTH, KV_LORA, MLA_HEADS * (QK_NOPE + V_HEAD)), KV_LORA),
        "w_o": nrm(ks[9], (DEPTH, D_MIX, D_MODEL), D_MIX),
        "norm_ffn_g": gain(ks[10], D_MODEL),
        "w_up": nrm(ks[11], (DEPTH, D_MODEL, 2 * D_FF), D_MODEL),
        "ffn_conv_w": nrm(ks[12], (DEPTH, FFN_CONV_K, 2 * D_FF), FFN_CONV_K),
        "ffn_conv_b": 0.01 * jax.random.normal(ks[13], (DEPTH, 2 * D_FF), f32),
        "w_down": nrm(ks[14], (DEPTH, D_FF, D_MODEL), D_FF),
        "ple_norm_g": gain(ks[15], D_MODEL),
        "w_ple_gate": nrm(ks[16], (DEPTH, D_MODEL, D_MODEL), D_MODEL),
        "w_ple_proj": nrm(ks[17], (DEPTH, PLE_DIM, D_MODEL), PLE_DIM),
        "final_norm_g": 1.0 + 0.01 * jax.random.normal(ks[18], (D_MODEL,), f32),
    }


def reference(x, p, norm_mix_g, w_in, conv_w, q_norm_g, w_uq, kv_norm_g, w_ukv, w_o,
              norm_ffn_g, w_up, ffn_conv_w, ffn_conv_b, w_down, ple_norm_g,
              w_ple_gate, w_ple_proj, final_norm_g):
    b, s, _ = x.shape
    cos, sin = rope_tables(s, QK_ROPE, x.dtype)
    split_pts = np.cumsum([CONV_WIDTH, CONV_WIDTH, CONV_WIDTH, Q_LORA, KV_LORA])

    for i in range(DEPTH):
        h = rmsnorm(x, norm_mix_g[i])
        z = h @ w_in[i]
        xc, bg, cg, q_lat, kv_lat, k_r = jnp.split(z, split_pts, axis=-1)

        y_conv = bg * dwconv_centred(cg * xc, conv_w[i])

        q = (rmsnorm(q_lat, q_norm_g[i]) @ w_uq[i]).reshape(b, s, MLA_HEADS, QK_NOPE + QK_ROPE)
        q_nope, q_rope = q[..., :QK_NOPE], q[..., QK_NOPE:]
        q_rope = q_rope * cos[None, :, None, :] + rotate_half(q_rope) * sin[None, :, None, :]
        kv = (rmsnorm(kv_lat, kv_norm_g[i]) @ w_ukv[i]).reshape(b, s, MLA_HEADS, QK_NOPE + V_HEAD)
        k_nope, v = kv[..., :QK_NOPE], kv[..., QK_NOPE:]
        k_rope = k_r * cos[None] + rotate_half(k_r) * sin[None]
        y_mla = mla_attention(q_nope, q_rope, k_nope, k_rope, v)

        x = x + jnp.concatenate([y_conv, y_mla], axis=-1) @ w_o[i]

        hf = rmsnorm(x, norm_ffn_g[i])
        a = dwconv_centred(hf @ w_up[i], ffn_conv_w[i], ffn_conv_b[i])
        g, u = jnp.split(a, 2, axis=-1)
        x = x + (jax.nn.silu(g) * u) @ w_down[i]

        gate = jax.nn.sigmoid(rmsnorm(x, ple_norm_g[i]) @ w_ple_gate[i])
        x = x + gate * (p[i] @ w_ple_proj[i])

    return rmsnorm(x, final_norm_g)
```

```python
import functools
import math

import jax
import jax.numpy as jnp
from jax import lax
from jax.experimental import pallas as pl
from jax.experimental.pallas import tpu as pltpu

D_MODEL = 1024
PLE_DIM = 256
CONV_WIDTH = 512
MLA_HEADS = 8
Q_LORA = 256
KV_LORA = 128
QK_NOPE = 64
QK_ROPE = 32
V_HEAD = 64
MLA_WIDTH = MLA_HEADS * V_HEAD
D_FF = 2816
ROPE_THETA = 10000.0
EPS = 1e-6

HEAD_PAD = 128
FF_CHUNK = 256
N_FF_CHUNKS = D_FF // FF_CHUNK
F32_SUBLANES = 8
BF16_SUBLANES = 16
VMEM_LIMIT_BYTES = 56 * 1024 * 1024

BF16 = jnp.bfloat16
F32 = jnp.float32


def _rms(x, g):
    return x * lax.rsqrt(jnp.mean(x * x, axis=-1, keepdims=True) + EPS) * g


def _resident(shape):
    zeros = (0,) * len(shape)
    return pl.BlockSpec(shape, lambda *_: zeros, pipeline_mode=pl.Buffered(1))


def _proj_kernel(xp_ref, xm_ref, xn_ref, gmix_ref, win_ref, convw_ref, gq_ref, wq_ref,
                 gkv_ref, wkv_ref, cos_ref, sin_ref,
                 yconv_ref, q_ref, k_ref, v_ref, *, tm, tiles_per_seq, q_scale):
    i = pl.program_id(0)
    halo = F32_SUBLANES
    x_ext = jnp.concatenate([xp_ref[...], xm_ref[...], xn_ref[...]], axis=0)
    h = _rms(x_ext, gmix_ref[...]).astype(BF16)
    z = jnp.dot(h, win_ref[...], preferred_element_type=F32)

    first = (i % tiles_per_seq) == 0
    last = (i % tiles_per_seq) == tiles_per_seq - 1
    lo = jnp.where(first, halo, 0)
    hi = jnp.where(last, tm + halo, tm + 2 * halo)
    row = lax.broadcasted_iota(jnp.int32, (tm + 2 * halo, 1), 0)
    u = z[:, 2 * CONV_WIDTH:3 * CONV_WIDTH] * z[:, 0:CONV_WIDTH]
    u = jnp.where((row >= lo) & (row < hi), u, 0.0)
    n_ext = tm + 2 * halo
    u_prev = pltpu.roll(u, 1, 0)[halo:halo + tm]
    u_next = pltpu.roll(u, n_ext - 1, 0)[halo:halo + tm]
    cw = convw_ref[...]
    conv = u_prev * cw[0:1] + u[halo:halo + tm] * cw[1:2] + u_next * cw[2:3]
    zm = z[halo:halo + tm]
    yconv_ref[...] = (zm[:, CONV_WIDTH:2 * CONV_WIDTH] * conv).astype(BF16)

    o = 3 * CONV_WIDTH
    q_lat = zm[:, o:o + Q_LORA]
    kv_lat = zm[:, o + Q_LORA:o + Q_LORA + KV_LORA]
    o2 = o + Q_LORA + KV_LORA
    kr = zm[:, o2:o2 + HEAD_PAD]
    kr_sw = zm[:, o2 + HEAD_PAD:o2 + 2 * HEAD_PAD]

    cos_k = cos_ref[...]
    sin_k = sin_ref[...]
    lane = lax.broadcasted_iota(jnp.int32, (1, HEAD_PAD), 1)
    cos_q = (cos_k + jnp.where(lane < QK_NOPE, 1.0, 0.0)) * q_scale
    sin_q = sin_k * q_scale

    qn = _rms(q_lat, gq_ref[...]).astype(BF16)
    qab = jnp.dot(qn, wq_ref[...], preferred_element_type=F32)
    hw = MLA_HEADS * HEAD_PAD
    for hd in range(MLA_HEADS):
        sl = slice(hd * HEAD_PAD, (hd + 1) * HEAD_PAD)
        sl2 = slice(hw + hd * HEAD_PAD, hw + (hd + 1) * HEAD_PAD)
        q_ref[:, sl] = (qab[:, sl] * cos_q + qab[:, sl2] * sin_q).astype(BF16)

    kvn = _rms(kv_lat, gkv_ref[...]).astype(BF16)
    kvo = jnp.dot(kvn, wkv_ref[...], preferred_element_type=F32)
    k_rope = kr * cos_k + kr_sw * sin_k
    for hd in range(MLA_HEADS):
        sl = slice(hd * HEAD_PAD, (hd + 1) * HEAD_PAD)
        k_ref[:, sl] = (kvo[:, sl] + k_rope).astype(BF16)
    v_ref[...] = kvo[:, hw:hw + MLA_WIDTH].astype(BF16)


def _proj_call(x2d, gmix, win, convw, gq, wq, gkv, wkv, cos_k, sin_k, *, seq, tm):
    t = x2d.shape[0]
    n_tiles = t // tm
    tiles_per_seq = seq // tm
    hb = tm // F32_SUBLANES
    n_hblocks = t // F32_SUBLANES
    q_scale = (QK_NOPE + QK_ROPE) ** -0.5 * math.log2(math.e)
    kern = functools.partial(_proj_kernel, tm=tm, tiles_per_seq=tiles_per_seq, q_scale=q_scale)
    row_spec = lambda w: pl.BlockSpec((tm, w), lambda i: (i, 0))
    return pl.pallas_call(
        kern,
        grid=(n_tiles,),
        in_specs=[
            pl.BlockSpec((F32_SUBLANES, D_MODEL), lambda i: (jnp.maximum(i * hb - 1, 0), 0)),
            row_spec(D_MODEL),
            pl.BlockSpec((F32_SUBLANES, D_MODEL), lambda i: (jnp.minimum((i + 1) * hb, n_hblocks - 1), 0)),
            _resident(gmix.shape), _resident(win.shape), _resident(convw.shape),
            _resident(gq.shape), _resident(wq.shape), _resident(gkv.shape), _resident(wkv.shape),
            pl.BlockSpec((tm, HEAD_PAD), lambda i: (i % tiles_per_seq, 0)),
            pl.BlockSpec((tm, HEAD_PAD), lambda i: (i % tiles_per_seq, 0)),
        ],
        out_specs=[row_spec(CONV_WIDTH), row_spec(MLA_HEADS * HEAD_PAD),
                   row_spec(MLA_HEADS * HEAD_PAD), row_spec(MLA_WIDTH)],
        out_shape=[jax.ShapeDtypeStruct((t, CONV_WIDTH), BF16),
                   jax.ShapeDtypeStruct((t, MLA_HEADS * HEAD_PAD), BF16),
                   jax.ShapeDtypeStruct((t, MLA_HEADS * HEAD_PAD), BF16),
                   jax.ShapeDtypeStruct((t, MLA_WIDTH), BF16)],
        compiler_params=pltpu.CompilerParams(dimension_semantics=("arbitrary",),
                                             vmem_limit_bytes=VMEM_LIMIT_BYTES),
        name="proj",
    )(x2d, x2d, x2d, gmix, win, convw, gq, wq, gkv, wkv, cos_k, sin_k)


def _attn_kernel(q_ref, k_ref, v_ref, o_ref, *, tq, tk, seq):
    outs = []
    for j in range(2):
        q = q_ref[:, j * HEAD_PAD:(j + 1) * HEAD_PAD]

        def body(c, carry, j=j, q=q):
            m, l, acc = carry
            off = pl.multiple_of(c * tk, tk)
            k = k_ref[pl.ds(off, tk), j * HEAD_PAD:(j + 1) * HEAD_PAD]
            v = v_ref[pl.ds(off, tk), j * V_HEAD:(j + 1) * V_HEAD]
            s = lax.dot_general(q, k, (((1,), (1,)), ((), ())), preferred_element_type=F32)
            m_new = jnp.maximum(m, jnp.max(s, axis=-1, keepdims=True))
            alpha = jnp.exp2(m - m_new)
            p = jnp.exp2(s - m_new)
            l = alpha * l + jnp.sum(p, axis=-1, keepdims=True)
            acc = alpha * acc + jnp.dot(p.astype(BF16), v, preferred_element_type=F32)
            return m_new, l, acc

        init = (jnp.full((tq, 1), -jnp.inf, F32), jnp.zeros((tq, 1), F32),
                jnp.zeros((tq, V_HEAD), F32))
        _, l, acc = lax.fori_loop(0, seq // tk, body, init)
        outs.append(acc / l)
    o_ref[...] = jnp.concatenate(outs, axis=-1).astype(BF16)


def _attn_call(q, k, v, *, batch, seq, tq, tk):
    t = q.shape[0]
    nq = seq // tq
    kern = functools.partial(_attn_kernel, tq=tq, tk=tk, seq=seq)
    return pl.pallas_call(
        kern,
        grid=(batch, MLA_HEADS // 2, nq),
        in_specs=[
            pl.BlockSpec((tq, 2 * HEAD_PAD), lambda b, hp, qi: (b * nq + qi, hp)),
            pl.BlockSpec((seq, 2 * HEAD_PAD), lambda b, hp, qi: (b, hp)),
            pl.BlockSpec((seq, 2 * V_HEAD), lambda b, hp, qi: (b, hp)),
        ],
        out_specs=pl.BlockSpec((tq, 2 * V_HEAD), lambda b, hp, qi: (b * nq + qi, hp)),
        out_shape=jax.ShapeDtypeStruct((t, MLA_WIDTH), BF16),
        compiler_params=pltpu.CompilerParams(
            dimension_semantics=("arbitrary", "arbitrary", "arbitrary"),
            vmem_limit_bytes=VMEM_LIMIT_BYTES),
        name="attn",
    )(q, k, v)


def _oproj_kernel(x_ref, yc_ref, ym_ref, woc_ref, wom_ref, gffn_ref, x1_ref, hf_ref):
    x1 = (x_ref[...]
          + jnp.dot(yc_ref[...], woc_ref[...], preferred_element_type=F32)
          + jnp.dot(ym_ref[...], wom_ref[...], preferred_element_type=F32))
    x1_ref[...] = x1
    hf_ref[...] = _rms(x1, gffn_ref[...]).astype(BF16)


def _oproj_call(x2d, yc, ym, woc, wom, gffn, *, tm):
    t = x2d.shape[0]
    row_spec = lambda w: pl.BlockSpec((tm, w), lambda i: (i, 0))
    return pl.pallas_call(
        _oproj_kernel,
        grid=(t // tm,),
        in_specs=[row_spec(D_MODEL), row_spec(CONV_WIDTH), row_spec(MLA_WIDTH),
                  _resident(woc.shape), _resident(wom.shape), _resident(gffn.shape)],
        out_specs=[row_spec(D_MODEL), row_spec(D_MODEL)],
        out_shape=[jax.ShapeDtypeStruct((t, D_MODEL), F32),
                   jax.ShapeDtypeStruct((t, D_MODEL), BF16)],
        compiler_params=pltpu.CompilerParams(dimension_semantics=("arbitrary",),
                                             vmem_limit_bytes=VMEM_LIMIT_BYTES),
        name="oproj",
    )(x2d, yc, ym, woc, wom, gffn)


def _ffn_kernel(hp_ref, hm_ref, hn_ref, x1_ref, p_ref, wup_ref, cw_ref, cb_ref, wdown_ref,
                gple_ref, wgate_ref, wproj_ref, gfin_ref, o_ref, acc_ref, *, tm, tiles_per_seq):
    i = pl.program_id(0)
    halo = BF16_SUBLANES
    n_ext = tm + 2 * halo
    h_ext = jnp.concatenate([hp_ref[...], hm_ref[...], hn_ref[...]], axis=0)

    first = (i % tiles_per_seq) == 0
    last = (i % tiles_per_seq) == tiles_per_seq - 1
    lo = jnp.where(first, halo, 0)
    hi = jnp.where(last, tm + halo, n_ext)
    row = lax.broadcasted_iota(jnp.int32, (n_ext, 1), 0)
    valid = (row >= lo) & (row < hi)

    def conv_half(c):
        a = jnp.dot(h_ext, wup_ref[c], preferred_element_type=F32)
        a = jnp.where(valid, a, 0.0)
        w = cw_ref[c]
        a_prev = pltpu.roll(a, 1, 0)[halo:halo + tm]
        a_next = pltpu.roll(a, n_ext - 1, 0)[halo:halo + tm]
        return a_prev * w[0:1] + a[halo:halo + tm] * w[1:2] + a_next * w[2:3] + cb_ref[c]

    acc_ref[...] = jnp.zeros_like(acc_ref)

    def body(c, carry):
        g = conv_half(c)
        u = conv_half(c + N_FF_CHUNKS)
        act = (g * jax.nn.sigmoid(g) * u).astype(BF16)
        acc_ref[...] += jnp.dot(act, wdown_ref[c], preferred_element_type=F32)
        return carry

    lax.fori_loop(0, N_FF_CHUNKS, body, 0)

    x2 = x1_ref[...] + acc_ref[...]
    gate = jax.nn.sigmoid(jnp.dot(_rms(x2, gple_ref[...]).astype(BF16), wgate_ref[...],
                                  preferred_element_type=F32))
    pp = jnp.dot(p_ref[...].astype(BF16), wproj_ref[...], preferred_element_type=F32)
    x3 = x2 + gate * pp
    o_ref[...] = _rms(x3, gfin_ref[...])


def _ffn_call(hf, x1, p2d, wup, cw, cb, wdown, gple, wgate, wproj, gfin, *, seq, tm):
    t = hf.shape[0]
    tiles_per_seq = seq // tm
    hb = tm // BF16_SUBLANES
    n_hblocks = t // BF16_SUBLANES
    kern = functools.partial(_ffn_kernel, tm=tm, tiles_per_seq=tiles_per_seq)
    row_spec = lambda w: pl.BlockSpec((tm, w), lambda i: (i, 0))
    return pl.pallas_call(
        kern,
        grid=(t // tm,),
        in_specs=[
            pl.BlockSpec((BF16_SUBLANES, D_MODEL), lambda i: (jnp.maximum(i * hb - 1, 0), 0)),
            row_spec(D_MODEL),
            pl.BlockSpec((BF16_SUBLANES, D_MODEL), lambda i: (jnp.minimum((i + 1) * hb, n_hblocks - 1), 0)),
            row_spec(D_MODEL), row_spec(PLE_DIM),
            _resident(wup.shape), _resident(cw.shape), _resident(cb.shape), _resident(wdown.shape),
            _resident(gple.shape), _resident(wgate.shape), _resident(wproj.shape), _resident(gfin.shape),
        ],
        out_specs=row_spec(D_MODEL),
        out_shape=jax.ShapeDtypeStruct((t, D_MODEL), F32),
        scratch_shapes=[pltpu.VMEM((tm, D_MODEL), F32)],
        compiler_params=pltpu.CompilerParams(dimension_semantics=("arbitrary",),
                                             vmem_limit_bytes=VMEM_LIMIT_BYTES),
        name="ffn",
    )(hf, hf, hf, x1, p2d, wup, cw, cb, wdown, gple, wgate, wproj, gfin)


def _swap_halves(w):
    half = w.shape[-1] // 2
    return jnp.concatenate([w[..., half:], w[..., :half]], axis=-1)


def _rope_tables(seq):
    pos = jnp.arange(seq, dtype=F32)
    inv_freq = ROPE_THETA ** (-jnp.arange(0, QK_ROPE, 2, dtype=F32) / QK_ROPE)
    ang = pos[:, None] * inv_freq[None, :]
    cos = jnp.cos(jnp.concatenate([ang, ang], axis=-1))
    sin = jnp.sin(jnp.concatenate([-ang, ang], axis=-1))
    pad = lambda a: jnp.pad(a, ((0, 0), (QK_NOPE, HEAD_PAD - QK_NOPE - QK_ROPE)))
    return pad(cos), pad(sin)


def _layout_weights(w_in, w_uq, w_ukv):
    d_conv_q_kv = 3 * CONV_WIDTH + Q_LORA + KV_LORA
    w_kr = w_in[:, d_conv_q_kv:]
    lane_pad = ((0, 0), (QK_NOPE, HEAD_PAD - QK_NOPE - QK_ROPE))
    win = jnp.concatenate([w_in[:, :d_conv_q_kv], jnp.pad(w_kr, lane_pad),
                           jnp.pad(_swap_halves(w_kr), lane_pad)], axis=1)

    wq3 = w_uq.reshape(Q_LORA, MLA_HEADS, QK_NOPE + QK_ROPE)
    nope, rope = wq3[..., :QK_NOPE], wq3[..., QK_NOPE:]
    zpad = jnp.zeros((Q_LORA, MLA_HEADS, HEAD_PAD - QK_NOPE - QK_ROPE), w_uq.dtype)
    wq_a = jnp.concatenate([nope, rope, zpad], axis=-1).reshape(Q_LORA, MLA_HEADS * HEAD_PAD)
    wq_b = jnp.concatenate([jnp.zeros_like(nope), _swap_halves(rope), zpad], axis=-1)
    wq = jnp.concatenate([wq_a, wq_b.reshape(Q_LORA, MLA_HEADS * HEAD_PAD)], axis=1)

    wkv3 = w_ukv.reshape(KV_LORA, MLA_HEADS, QK_NOPE + V_HEAD)
    k_nope, v = wkv3[..., :QK_NOPE], wkv3[..., QK_NOPE:]
    wk = jnp.pad(k_nope, ((0, 0), (0, 0), (0, HEAD_PAD - QK_NOPE))).reshape(KV_LORA, MLA_HEADS * HEAD_PAD)
    wkv = jnp.concatenate([wk, v.reshape(KV_LORA, MLA_WIDTH)], axis=1)
    return win.astype(BF16), wq.astype(BF16), wkv.astype(BF16)


def _chunk_cols(w):
    r, c = w.shape
    return w.reshape(r, c // FF_CHUNK, FF_CHUNK).transpose(1, 0, 2)


def kernel(x, p, norm_mix_g, w_in, conv_w, q_norm_g, w_uq, kv_norm_g, w_ukv, w_o, norm_ffn_g, w_up, ffn_conv_w, ffn_conv_b, w_down, ple_norm_g, w_ple_gate, w_ple_proj, final_norm_g):
    batch, seq, d = x.shape
    depth = w_in.shape[0]
    t = batch * seq
    tm = 512
    assert seq % tm == 0 and d == D_MODEL and depth == 1
    cos_k, sin_k = _rope_tables(seq)
    row = lambda g: g.reshape(1, -1)

    x2d = x.reshape(t, d)
    for i in range(depth):
        win, wq, wkv = _layout_weights(w_in[i], w_uq[i], w_ukv[i])
        yc, q, k, v = _proj_call(x2d, row(norm_mix_g[i]), win, conv_w[i], row(q_norm_g[i]), wq,
                                 row(kv_norm_g[i]), wkv, cos_k, sin_k, seq=seq, tm=tm)
        ym = _attn_call(q, k, v, batch=batch, seq=seq, tq=256, tk=512)
        wo = w_o[i].astype(BF16)
        x1, hf = _oproj_call(x2d, yc, ym, wo[:CONV_WIDTH], wo[CONV_WIDTH:], row(norm_ffn_g[i]), tm=tm)
        x2d = _ffn_call(hf, x1, p[i].reshape(t, PLE_DIM),
                        _chunk_cols(w_up[i].astype(BF16)), _chunk_cols(ffn_conv_w[i]),
                        _chunk_cols(ffn_conv_b[i].reshape(1, -1)),
                        w_down[i].astype(BF16).reshape(N_FF_CHUNKS, FF_CHUNK, D_MODEL),
                        row(ple_norm_g[i]), w_ple_gate[i].astype(BF16), w_ple_proj[i].astype(BF16),
                        row(final_norm_g), seq=seq, tm=tm)
    return x2d.reshape(batch, seq, d)
```

```python
import functools
import math

import jax
import jax.numpy as jnp
from jax import lax
from jax.experimental import pallas as pl
from jax.experimental.pallas import tpu as pltpu

D_MODEL = 1024
PLE_DIM = 256
CONV_WIDTH = 512
MLA_HEADS = 8
Q_LORA = 256
KV_LORA = 128
QK_NOPE = 64
QK_ROPE = 32
V_HEAD = 64
MLA_WIDTH = MLA_HEADS * V_HEAD
D_FF = 2816
ROPE_THETA = 10000.0
EPS = 1e-6

HEAD_PAD = 128
FF_CHUNK = 256
N_FF_CHUNKS = D_FF // FF_CHUNK
F32_SUBLANES = 8
BF16_SUBLANES = 16
V_AUG = V_HEAD + BF16_SUBLANES
VMEM_LIMIT_BYTES = 56 * 1024 * 1024

BF16 = jnp.bfloat16
F32 = jnp.float32
_NT = (((1,), (1,)), ((), ()))
_TN = (((0,), (0,)), ((), ()))


def _rms(x, g):
    return x * lax.rsqrt(jnp.mean(x * x, axis=-1, keepdims=True) + EPS) * g


def _resident(shape):
    zeros = (0,) * len(shape)
    return pl.BlockSpec(shape, lambda *_: zeros, pipeline_mode=pl.Buffered(1))


def _proj_kernel(xp_ref, xm_ref, xn_ref, gmix_ref, win_ref, convw_ref, gq_ref, wqt_ref,
                 gkv_ref, wk_ref, wvt_ref, cos_ref, sin_ref, cost_ref, sint_ref,
                 yconv_ref, qt_ref, k_ref, vt_ref, *, tm, tiles_per_seq, q_scale):
    i = pl.program_id(0)
    halo = F32_SUBLANES
    x_ext = jnp.concatenate([xp_ref[...], xm_ref[...], xn_ref[...]], axis=0)
    h = _rms(x_ext, gmix_ref[...]).astype(BF16)
    z = jnp.dot(h, win_ref[...], preferred_element_type=F32)

    first = (i % tiles_per_seq) == 0
    last = (i % tiles_per_seq) == tiles_per_seq - 1
    lo = jnp.where(first, halo, 0)
    hi = jnp.where(last, tm + halo, tm + 2 * halo)
    row = lax.broadcasted_iota(jnp.int32, (tm + 2 * halo, 1), 0)
    u = z[:, 2 * CONV_WIDTH:3 * CONV_WIDTH] * z[:, 0:CONV_WIDTH]
    u = jnp.where((row >= lo) & (row < hi), u, 0.0)
    n_ext = tm + 2 * halo
    u_prev = pltpu.roll(u, 1, 0)[halo:halo + tm]
    u_next = pltpu.roll(u, n_ext - 1, 0)[halo:halo + tm]
    cw = convw_ref[...]
    conv = u_prev * cw[0:1] + u[halo:halo + tm] * cw[1:2] + u_next * cw[2:3]
    zm = z[halo:halo + tm]
    yconv_ref[...] = (zm[:, CONV_WIDTH:2 * CONV_WIDTH] * conv).astype(BF16)

    o = 3 * CONV_WIDTH
    q_lat = zm[:, o:o + Q_LORA]
    kv_lat = zm[:, o + Q_LORA:o + Q_LORA + KV_LORA]
    o2 = o + Q_LORA + KV_LORA
    kr = zm[:, o2:o2 + HEAD_PAD]
    kr_sw = zm[:, o2 + HEAD_PAD:o2 + 2 * HEAD_PAD]

    qn = _rms(q_lat, gq_ref[...]).astype(BF16)
    qab_t = lax.dot_general(wqt_ref[...], qn, _NT, preferred_element_type=F32)
    sub = lax.broadcasted_iota(jnp.int32, (HEAD_PAD, 1), 0)
    cos_qt = (cost_ref[...] + jnp.where(sub < QK_NOPE, 1.0, 0.0)) * q_scale
    sin_qt = sint_ref[...] * q_scale
    hw = MLA_HEADS * HEAD_PAD
    for hd in range(MLA_HEADS):
        sl = slice(hd * HEAD_PAD, (hd + 1) * HEAD_PAD)
        sl2 = slice(hw + hd * HEAD_PAD, hw + (hd + 1) * HEAD_PAD)
        qt_ref[0, sl, :] = (qab_t[sl] * cos_qt + qab_t[sl2] * sin_qt).astype(BF16)

    kvn = _rms(kv_lat, gkv_ref[...]).astype(BF16)
    ko = jnp.dot(kvn, wk_ref[...], preferred_element_type=F32)
    k_rope = kr * cos_ref[...] + kr_sw * sin_ref[...]
    for hd in range(MLA_HEADS):
        sl = slice(hd * HEAD_PAD, (hd + 1) * HEAD_PAD)
        k_ref[:, sl] = (ko[:, sl] + k_rope).astype(BF16)
    vt = lax.dot_general(wvt_ref[...], kvn, _NT, preferred_element_type=F32).astype(BF16)
    ones_rows = (lax.broadcasted_iota(jnp.int32, (V_AUG - V_HEAD, tm), 0) == 0).astype(BF16)
    for hd in range(MLA_HEADS):
        vt_ref[0, hd * V_AUG:hd * V_AUG + V_HEAD, :] = vt[hd * V_HEAD:(hd + 1) * V_HEAD]
        vt_ref[0, hd * V_AUG + V_HEAD:(hd + 1) * V_AUG, :] = ones_rows


def _proj_call(x2d, gmix, win, convw, gq, wqt, gkv, wk, wvt, cos_k, sin_k, cos_t, sin_t, *, seq, tm):
    t = x2d.shape[0]
    n_tiles = t // tm
    tiles_per_seq = seq // tm
    hb = tm // F32_SUBLANES
    n_hblocks = t // F32_SUBLANES
    hw = MLA_HEADS * HEAD_PAD
    q_scale = (QK_NOPE + QK_ROPE) ** -0.5 * math.log2(math.e)
    kern = functools.partial(_proj_kernel, tm=tm, tiles_per_seq=tiles_per_seq, q_scale=q_scale)
    row_spec = lambda w: pl.BlockSpec((tm, w), lambda i: (i, 0))
    return pl.pallas_call(
        kern,
        grid=(n_tiles,),
        in_specs=[
            pl.BlockSpec((F32_SUBLANES, D_MODEL), lambda i: (jnp.maximum(i * hb - 1, 0), 0)),
            row_spec(D_MODEL),
            pl.BlockSpec((F32_SUBLANES, D_MODEL), lambda i: (jnp.minimum((i + 1) * hb, n_hblocks - 1), 0)),
            _resident(gmix.shape), _resident(win.shape), _resident(convw.shape),
            _resident(gq.shape), _resident(wqt.shape), _resident(gkv.shape),
            _resident(wk.shape), _resident(wvt.shape),
            pl.BlockSpec((tm, HEAD_PAD), lambda i: (i % tiles_per_seq, 0)),
            pl.BlockSpec((tm, HEAD_PAD), lambda i: (i % tiles_per_seq, 0)),
            pl.BlockSpec((HEAD_PAD, tm), lambda i: (0, i % tiles_per_seq)),
            pl.BlockSpec((HEAD_PAD, tm), lambda i: (0, i % tiles_per_seq)),
        ],
        out_specs=[row_spec(CONV_WIDTH),
                   pl.BlockSpec((1, hw, tm), lambda i: (i, 0, 0)),
                   row_spec(hw),
                   pl.BlockSpec((1, MLA_HEADS * V_AUG, tm), lambda i: (i, 0, 0))],
        out_shape=[jax.ShapeDtypeStruct((t, CONV_WIDTH), BF16),
                   jax.ShapeDtypeStruct((n_tiles, hw, tm), BF16),
                   jax.ShapeDtypeStruct((t, hw), BF16),
                   jax.ShapeDtypeStruct((n_tiles, MLA_HEADS * V_AUG, tm), BF16)],
        compiler_params=pltpu.CompilerParams(dimension_semantics=("arbitrary",),
                                             vmem_limit_bytes=VMEM_LIMIT_BYTES),
        name="proj",
    )(x2d, x2d, x2d, gmix, win, convw, gq, wqt, gkv, wk, wvt, cos_k, sin_k, cos_t, sin_t)


def _attn_kernel(qt_ref, k_ref, vt_ref, o_ref, s_ref, p_ref, acc_ref, *, tk, tq, tm, n_chunks, n_qtiles):
    sub_q = tq // tm
    sub_k = tk // tm
    shift = n_chunks.bit_length() - 1
    assert 1 << shift == n_chunks
    n_pairs = n_qtiles * n_chunks

    def scores(t):
        c = t & (n_chunks - 1)
        qi = t >> shift
        qt = jnp.concatenate([qt_ref[qi * sub_q + j] for j in range(sub_q)], axis=1)
        off = pl.multiple_of(c * tk, tk)
        s = jnp.dot(k_ref[pl.ds(off, tk), :], qt, preferred_element_type=F32)
        s_ref[...] = s
        return jnp.max(s, axis=0, keepdims=True)

    def probs(t, m, mc):
        m = jnp.where((t & (n_chunks - 1)) == 0, -jnp.inf, m)
        m_new = jnp.maximum(m, mc)
        p_ref[...] = jnp.exp2(s_ref[...] - m_new).astype(BF16)
        return m_new, jnp.exp2(m - m_new)

    def weighted_values(t, alpha):
        c = t & (n_chunks - 1)
        qi = t >> shift
        vt = jnp.concatenate([vt_ref[c * sub_k + j] for j in range(sub_k)], axis=1)
        acc = alpha * acc_ref[...] + jnp.dot(vt, p_ref[...], preferred_element_type=F32)
        acc_ref[...] = acc
        out = (acc[:V_HEAD] / acc[V_HEAD:V_HEAD + 1]).astype(BF16)
        for j in range(sub_q):
            o_ref[qi * sub_q + j] = out[:, j * tm:(j + 1) * tm]

    acc_ref[...] = jnp.zeros_like(acc_ref)
    m = jnp.zeros((1, tq), F32)
    mc = scores(0)
    m, alpha = probs(0, m, mc)
    mc = scores(1)

    def step(t, carry):
        m, mc, alpha_prev = carry
        weighted_values(t - 1, alpha_prev)
        m, alpha = probs(t, m, mc)
        mc = scores(t + 1)
        return m, mc, alpha

    m, mc, alpha_prev = lax.fori_loop(1, n_pairs - 1, step, (m, mc, alpha))
    weighted_values(n_pairs - 2, alpha_prev)
    m, alpha = probs(n_pairs - 1, m, mc)
    weighted_values(n_pairs - 1, alpha)


def _attn_call(qt, k, vt, *, batch, seq, tq, tk):
    n_tiles, _, tm = qt.shape
    tiles_per_seq = n_tiles // batch
    kern = functools.partial(_attn_kernel, tk=tk, tq=tq, tm=tm, n_chunks=seq // tk, n_qtiles=seq // tq)
    return pl.pallas_call(
        kern,
        grid=(batch, MLA_HEADS),
        in_specs=[
            pl.BlockSpec((tiles_per_seq, HEAD_PAD, tm), lambda b, h: (b, h, 0)),
            pl.BlockSpec((seq, HEAD_PAD), lambda b, h: (b, h)),
            pl.BlockSpec((tiles_per_seq, V_AUG, tm), lambda b, h: (b, h, 0)),
        ],
        out_specs=pl.BlockSpec((tiles_per_seq, V_HEAD, tm), lambda b, h: (b, h, 0)),
        out_shape=jax.ShapeDtypeStruct((n_tiles, MLA_WIDTH, tm), BF16),
        scratch_shapes=[pltpu.VMEM((tk, tq), F32), pltpu.VMEM((tk, tq), BF16),
                        pltpu.VMEM((V_AUG, tq), F32)],
        compiler_params=pltpu.CompilerParams(dimension_semantics=("arbitrary", "arbitrary"),
                                             vmem_limit_bytes=VMEM_LIMIT_BYTES),
        name="attn",
    )(qt, k, vt)


def _oproj_kernel(x_ref, yc_ref, ymt_ref, woc_ref, wom_ref, gffn_ref, x1_ref, hf_ref):
    x1 = (x_ref[...]
          + jnp.dot(yc_ref[...], woc_ref[...], preferred_element_type=F32)
          + lax.dot_general(ymt_ref[0], wom_ref[...], _TN, preferred_element_type=F32))
    x1_ref[...] = x1
    hf_ref[...] = _rms(x1, gffn_ref[...]).astype(BF16)


def _oproj_call(x2d, yc, ymt, woc, wom, gffn, *, tm):
    t = x2d.shape[0]
    row_spec = lambda w: pl.BlockSpec((tm, w), lambda i: (i, 0))
    return pl.pallas_call(
        _oproj_kernel,
        grid=(t // tm,),
        in_specs=[row_spec(D_MODEL), row_spec(CONV_WIDTH),
                  pl.BlockSpec((1, MLA_WIDTH, tm), lambda i: (i, 0, 0)),
                  _resident(woc.shape), _resident(wom.shape), _resident(gffn.shape)],
        out_specs=[row_spec(D_MODEL), row_spec(D_MODEL)],
        out_shape=[jax.ShapeDtypeStruct((t, D_MODEL), F32),
                   jax.ShapeDtypeStruct((t, D_MODEL), BF16)],
        compiler_params=pltpu.CompilerParams(dimension_semantics=("arbitrary",),
                                             vmem_limit_bytes=VMEM_LIMIT_BYTES),
        name="oproj",
    )(x2d, yc, ymt, woc, wom, gffn)


def _ffn_kernel(hp_ref, hm_ref, hn_ref, x1_ref, p_ref, wup_ref, cw_ref, cb_ref, wdown_ref,
                gple_ref, wgate_ref, wproj_ref, gfin_ref, o_ref, acc_ref, *, tm, tiles_per_seq):
    i = pl.program_id(0)
    halo = BF16_SUBLANES
    n_ext = tm + 2 * halo
    h_ext = jnp.concatenate([hp_ref[...], hm_ref[...], hn_ref[...]], axis=0)

    first = (i % tiles_per_seq) == 0
    last = (i % tiles_per_seq) == tiles_per_seq - 1
    lo = jnp.where(first, halo, 0)
    hi = jnp.where(last, tm + halo, n_ext)
    row = lax.broadcasted_iota(jnp.int32, (n_ext, 1), 0)
    valid = (row >= lo) & (row < hi)

    def conv_half(c):
        a = jnp.dot(h_ext, wup_ref[c], preferred_element_type=F32)
        a = jnp.where(valid, a, 0.0)
        w = cw_ref[c]
        a_prev = pltpu.roll(a, 1, 0)[halo:halo + tm]
        a_next = pltpu.roll(a, n_ext - 1, 0)[halo:halo + tm]
        return a_prev * w[0:1] + a[halo:halo + tm] * w[1:2] + a_next * w[2:3] + cb_ref[c]

    acc_ref[...] = jnp.zeros_like(acc_ref)

    def body(c, carry):
        g = conv_half(c)
        u = conv_half(c + N_FF_CHUNKS)
        act = (g * jax.nn.sigmoid(g) * u).astype(BF16)
        acc_ref[...] += jnp.dot(act, wdown_ref[c], preferred_element_type=F32)
        return carry

    lax.fori_loop(0, N_FF_CHUNKS, body, 0)

    x2 = x1_ref[...] + acc_ref[...]
    gate = jax.nn.sigmoid(jnp.dot(_rms(x2, gple_ref[...]).astype(BF16), wgate_ref[...],
                                  preferred_element_type=F32))
    pp = jnp.dot(p_ref[...].astype(BF16), wproj_ref[...], preferred_element_type=F32)
    x3 = x2 + gate * pp
    o_ref[...] = _rms(x3, gfin_ref[...])


def _ffn_call(hf, x1, p2d, wup, cw, cb, wdown, gple, wgate, wproj, gfin, *, seq, tm):
    t = hf.shape[0]
    tiles_per_seq = seq // tm
    hb = tm // BF16_SUBLANES
    n_hblocks = t // BF16_SUBLANES
    kern = functools.partial(_ffn_kernel, tm=tm, tiles_per_seq=tiles_per_seq)
    row_spec = lambda w: pl.BlockSpec((tm, w), lambda i: (i, 0))
    return pl.pallas_call(
        kern,
        grid=(t // tm,),
        in_specs=[
            pl.BlockSpec((BF16_SUBLANES, D_MODEL), lambda i: (jnp.maximum(i * hb - 1, 0), 0)),
            row_spec(D_MODEL),
            pl.BlockSpec((BF16_SUBLANES, D_MODEL), lambda i: (jnp.minimum((i + 1) * hb, n_hblocks - 1), 0)),
            row_spec(D_MODEL), row_spec(PLE_DIM),
            _resident(wup.shape), _resident(cw.shape), _resident(cb.shape), _resident(wdown.shape),
            _resident(gple.shape), _resident(wgate.shape), _resident(wproj.shape), _resident(gfin.shape),
        ],
        out_specs=row_spec(D_MODEL),
        out_shape=jax.ShapeDtypeStruct((t, D_MODEL), F32),
        scratch_shapes=[pltpu.VMEM((tm, D_MODEL), F32)],
        compiler_params=pltpu.CompilerParams(dimension_semantics=("arbitrary",),
                                             vmem_limit_bytes=VMEM_LIMIT_BYTES),
        name="ffn",
    )(hf, hf, hf, x1, p2d, wup, cw, cb, wdown, gple, wgate, wproj, gfin)


def _swap_halves(w):
    half = w.shape[-1] // 2
    return jnp.concatenate([w[..., half:], w[..., :half]], axis=-1)


def _rope_tables(seq):
    pos = jnp.arange(seq, dtype=F32)
    inv_freq = ROPE_THETA ** (-jnp.arange(0, QK_ROPE, 2, dtype=F32) / QK_ROPE)
    ang = pos[:, None] * inv_freq[None, :]
    cos = jnp.cos(jnp.concatenate([ang, ang], axis=-1))
    sin = jnp.sin(jnp.concatenate([-ang, ang], axis=-1))
    pad = lambda a: jnp.pad(a, ((0, 0), (QK_NOPE, HEAD_PAD - QK_NOPE - QK_ROPE)))
    return pad(cos), pad(sin)


def _layout_weights(w_in, w_uq, w_ukv):
    d_conv_q_kv = 3 * CONV_WIDTH + Q_LORA + KV_LORA
    w_kr = w_in[:, d_conv_q_kv:]
    lane_pad = ((0, 0), (QK_NOPE, HEAD_PAD - QK_NOPE - QK_ROPE))
    win = jnp.concatenate([w_in[:, :d_conv_q_kv], jnp.pad(w_kr, lane_pad),
                           jnp.pad(_swap_halves(w_kr), lane_pad)], axis=1)

    wq3 = w_uq.reshape(Q_LORA, MLA_HEADS, QK_NOPE + QK_ROPE)
    nope, rope = wq3[..., :QK_NOPE], wq3[..., QK_NOPE:]
    zpad = jnp.zeros((Q_LORA, MLA_HEADS, HEAD_PAD - QK_NOPE - QK_ROPE), w_uq.dtype)
    wq_a = jnp.concatenate([nope, rope, zpad], axis=-1).reshape(Q_LORA, MLA_HEADS * HEAD_PAD)
    wq_b = jnp.concatenate([jnp.zeros_like(nope), _swap_halves(rope), zpad], axis=-1)
    wq = jnp.concatenate([wq_a, wq_b.reshape(Q_LORA, MLA_HEADS * HEAD_PAD)], axis=1)

    wkv3 = w_ukv.reshape(KV_LORA, MLA_HEADS, QK_NOPE + V_HEAD)
    k_nope, v = wkv3[..., :QK_NOPE], wkv3[..., QK_NOPE:]
    wk = jnp.pad(k_nope, ((0, 0), (0, 0), (0, HEAD_PAD - QK_NOPE))).reshape(KV_LORA, MLA_HEADS * HEAD_PAD)
    wv = v.reshape(KV_LORA, MLA_WIDTH)
    return win.astype(BF16), wq.T.astype(BF16), wk.astype(BF16), wv.T.astype(BF16)


def _chunk_cols(w):
    r, c = w.shape
    return w.reshape(r, c // FF_CHUNK, FF_CHUNK).transpose(1, 0, 2)


def kernel(x, p, norm_mix_g, w_in, conv_w, q_norm_g, w_uq, kv_norm_g, w_ukv, w_o, norm_ffn_g, w_up, ffn_conv_w, ffn_conv_b, w_down, ple_norm_g, w_ple_gate, w_ple_proj, final_norm_g):
    batch, seq, d = x.shape
    depth = w_in.shape[0]
    t = batch * seq
    tm = 512
    assert seq % tm == 0 and d == D_MODEL and depth == 1
    cos_k, sin_k = _rope_tables(seq)
    row = lambda g: g.reshape(1, -1)

    x2d = x.reshape(t, d)
    for i in range(depth):
        win, wqt, wk, wvt = _layout_weights(w_in[i], w_uq[i], w_ukv[i])
        yc, qt, k, vt = _proj_call(x2d, row(norm_mix_g[i]), win, conv_w[i], row(q_norm_g[i]), wqt,
                                   row(kv_norm_g[i]), wk, wvt, cos_k, sin_k, cos_k.T, sin_k.T,
                                   seq=seq, tm=tm)
        ymt = _attn_call(qt, k, vt, batch=batch, seq=seq, tq=1024, tk=512)
        wo = w_o[i].astype(BF16)
        x1, hf = _oproj_call(x2d, yc, ymt, wo[:CONV_WIDTH], wo[CONV_WIDTH:], row(norm_ffn_g[i]), tm=tm)
        x2d = _ffn_call(hf, x1, p[i].reshape(t, PLE_DIM),
                        _chunk_cols(w_up[i].astype(BF16)), _chunk_cols(ffn_conv_w[i]),
                        _chunk_cols(ffn_conv_b[i].reshape(1, -1)),
                        w_down[i].astype(BF16).reshape(N_FF_CHUNKS, FF_CHUNK, D_MODEL),
                        row(ple_norm_g[i]), w_ple_gate[i].astype(BF16), w_ple_proj[i].astype(BF16),
                        row(final_norm_g), seq=seq, tm=tm)
    return x2d.reshape(batch, seq, d)
```

```python
import functools
import math

import jax
import jax.numpy as jnp
from jax import lax
from jax.experimental import pallas as pl
from jax.experimental.pallas import tpu as pltpu

D_MODEL = 1024
PLE_DIM = 256
CONV_WIDTH = 512
MLA_HEADS = 8
Q_LORA = 256
KV_LORA = 128
QK_NOPE = 64
QK_ROPE = 32
V_HEAD = 64
MLA_WIDTH = MLA_HEADS * V_HEAD
D_FF = 2816
ROPE_THETA = 10000.0
EPS = 1e-6

HEAD_PAD = 128
FF_CHUNK = 256
N_FF_CHUNKS = D_FF // FF_CHUNK
F32_SUBLANES = 8
BF16_SUBLANES = 16
V_AUG = V_HEAD + BF16_SUBLANES
VMEM_LIMIT_BYTES = 56 * 1024 * 1024

BF16 = jnp.bfloat16
F32 = jnp.float32
_NT = (((1,), (1,)), ((), ()))
_TN = (((0,), (0,)), ((), ()))


def _rms(x, g):
    return x * lax.rsqrt(jnp.mean(x * x, axis=-1, keepdims=True) + EPS) * g


def _resident(shape):
    zeros = (0,) * len(shape)
    return pl.BlockSpec(shape, lambda *_: zeros, pipeline_mode=pl.Buffered(1))


def _proj_kernel(xp_ref, xm_ref, xn_ref, gmix_ref, win_ref, convw_ref, gq_ref, wqt_ref,
                 gkv_ref, wk_ref, wvt_ref, cos_ref, sin_ref, cost_ref, sint_ref,
                 yconv_ref, qt_ref, k_ref, vt_ref, *, tm, tiles_per_seq, q_scale):
    i = pl.program_id(0)
    halo = F32_SUBLANES
    x_ext = jnp.concatenate([xp_ref[...], xm_ref[...], xn_ref[...]], axis=0)
    h = _rms(x_ext, gmix_ref[...]).astype(BF16)
    z = jnp.dot(h, win_ref[...], preferred_element_type=F32)

    first = (i % tiles_per_seq) == 0
    last = (i % tiles_per_seq) == tiles_per_seq - 1
    lo = jnp.where(first, halo, 0)
    hi = jnp.where(last, tm + halo, tm + 2 * halo)
    row = lax.broadcasted_iota(jnp.int32, (tm + 2 * halo, 1), 0)
    u = z[:, 2 * CONV_WIDTH:3 * CONV_WIDTH] * z[:, 0:CONV_WIDTH]
    u = jnp.where((row >= lo) & (row < hi), u, 0.0)
    n_ext = tm + 2 * halo
    u_prev = pltpu.roll(u, 1, 0)[halo:halo + tm]
    u_next = pltpu.roll(u, n_ext - 1, 0)[halo:halo + tm]
    cw = convw_ref[...]
    conv = u_prev * cw[0:1] + u[halo:halo + tm] * cw[1:2] + u_next * cw[2:3]
    zm = z[halo:halo + tm]
    yconv_ref[...] = (zm[:, CONV_WIDTH:2 * CONV_WIDTH] * conv).astype(BF16)

    o = 3 * CONV_WIDTH
    q_lat = zm[:, o:o + Q_LORA]
    kv_lat = zm[:, o + Q_LORA:o + Q_LORA + KV_LORA]
    o2 = o + Q_LORA + KV_LORA
    kr = zm[:, o2:o2 + HEAD_PAD]
    kr_sw = zm[:, o2 + HEAD_PAD:o2 + 2 * HEAD_PAD]

    qn = _rms(q_lat, gq_ref[...]).astype(BF16)
    qab_t = lax.dot_general(wqt_ref[...], qn, _NT, preferred_element_type=F32)
    sub = lax.broadcasted_iota(jnp.int32, (HEAD_PAD, 1), 0)
    cos_qt = (cost_ref[...] + jnp.where(sub < QK_NOPE, 1.0, 0.0)) * q_scale
    sin_qt = sint_ref[...] * q_scale
    hw = MLA_HEADS * HEAD_PAD
    for hd in range(MLA_HEADS):
        sl = slice(hd * HEAD_PAD, (hd + 1) * HEAD_PAD)
        sl2 = slice(hw + hd * HEAD_PAD, hw + (hd + 1) * HEAD_PAD)
        qt_ref[0, sl, :] = (qab_t[sl] * cos_qt + qab_t[sl2] * sin_qt).astype(BF16)

    kvn = _rms(kv_lat, gkv_ref[...]).astype(BF16)
    ko = jnp.dot(kvn, wk_ref[...], preferred_element_type=F32)
    k_rope = kr * cos_ref[...] + kr_sw * sin_ref[...]
    for hd in range(MLA_HEADS):
        sl = slice(hd * HEAD_PAD, (hd + 1) * HEAD_PAD)
        k_ref[:, sl] = (ko[:, sl] + k_rope).astype(BF16)
    vt = lax.dot_general(wvt_ref[...], kvn, _NT, preferred_element_type=F32).astype(BF16)
    ones_rows = (lax.broadcasted_iota(jnp.int32, (V_AUG - V_HEAD, tm), 0) == 0).astype(BF16)
    for hd in range(MLA_HEADS):
        vt_ref[0, hd * V_AUG:hd * V_AUG + V_HEAD, :] = vt[hd * V_HEAD:(hd + 1) * V_HEAD]
        vt_ref[0, hd * V_AUG + V_HEAD:(hd + 1) * V_AUG, :] = ones_rows


def _proj_call(x2d, gmix, win, convw, gq, wqt, gkv, wk, wvt, cos_k, sin_k, cos_t, sin_t, *, seq, tm):
    t = x2d.shape[0]
    n_tiles = t // tm
    tiles_per_seq = seq // tm
    hb = tm // F32_SUBLANES
    n_hblocks = t // F32_SUBLANES
    hw = MLA_HEADS * HEAD_PAD
    q_scale = (QK_NOPE + QK_ROPE) ** -0.5 * math.log2(math.e)
    kern = functools.partial(_proj_kernel, tm=tm, tiles_per_seq=tiles_per_seq, q_scale=q_scale)
    row_spec = lambda w: pl.BlockSpec((tm, w), lambda i: (i, 0))
    return pl.pallas_call(
        kern,
        grid=(n_tiles,),
        in_specs=[
            pl.BlockSpec((F32_SUBLANES, D_MODEL), lambda i: (jnp.maximum(i * hb - 1, 0), 0)),
            row_spec(D_MODEL),
            pl.BlockSpec((F32_SUBLANES, D_MODEL), lambda i: (jnp.minimum((i + 1) * hb, n_hblocks - 1), 0)),
            _resident(gmix.shape), _resident(win.shape), _resident(convw.shape),
            _resident(gq.shape), _resident(wqt.shape), _resident(gkv.shape),
            _resident(wk.shape), _resident(wvt.shape),
            pl.BlockSpec((tm, HEAD_PAD), lambda i: (i % tiles_per_seq, 0)),
            pl.BlockSpec((tm, HEAD_PAD), lambda i: (i % tiles_per_seq, 0)),
            pl.BlockSpec((HEAD_PAD, tm), lambda i: (0, i % tiles_per_seq)),
            pl.BlockSpec((HEAD_PAD, tm), lambda i: (0, i % tiles_per_seq)),
        ],
        out_specs=[row_spec(CONV_WIDTH),
                   pl.BlockSpec((1, hw, tm), lambda i: (i, 0, 0)),
                   row_spec(hw),
                   pl.BlockSpec((1, MLA_HEADS * V_AUG, tm), lambda i: (i, 0, 0))],
        out_shape=[jax.ShapeDtypeStruct((t, CONV_WIDTH), BF16),
                   jax.ShapeDtypeStruct((n_tiles, hw, tm), BF16),
                   jax.ShapeDtypeStruct((t, hw), BF16),
                   jax.ShapeDtypeStruct((n_tiles, MLA_HEADS * V_AUG, tm), BF16)],
        compiler_params=pltpu.CompilerParams(dimension_semantics=("arbitrary",),
                                             vmem_limit_bytes=VMEM_LIMIT_BYTES),
        name="proj",
    )(x2d, x2d, x2d, gmix, win, convw, gq, wqt, gkv, wk, wvt, cos_k, sin_k, cos_t, sin_t)


def _attn_kernel(qt_ref, k_ref, vt_ref, o_ref, *scratch, tk, tq, tm, n_chunks, n_qtiles, n_streams):
    s_refs = scratch[0:n_streams]
    p_refs = scratch[n_streams:2 * n_streams]
    acc_refs = scratch[2 * n_streams:3 * n_streams]
    sub_q = tq // tm
    sub_k = tk // tm
    shift = n_chunks.bit_length() - 1
    assert 1 << shift == n_chunks
    n_pairs = n_qtiles * n_chunks

    def scores(g, t):
        c = t & (n_chunks - 1)
        qi = t >> shift
        hd = slice(g * HEAD_PAD, (g + 1) * HEAD_PAD)
        qt = jnp.concatenate([qt_ref[qi * sub_q + j, hd, :] for j in range(sub_q)], axis=1)
        off = pl.multiple_of(c * tk, tk)
        s = jnp.dot(k_ref[pl.ds(off, tk), hd], qt, preferred_element_type=F32)
        s_refs[g][...] = s
        return jnp.max(s, axis=0, keepdims=True)

    def probs(g, t, m, mc):
        m = jnp.where((t & (n_chunks - 1)) == 0, -jnp.inf, m)
        m_new = jnp.maximum(m, mc)
        p_refs[g][...] = jnp.exp2(s_refs[g][...] - m_new).astype(BF16)
        return m_new, jnp.exp2(m - m_new)

    def weighted_values(g, t, alpha):
        c = t & (n_chunks - 1)
        qi = t >> shift
        va = slice(g * V_AUG, (g + 1) * V_AUG)
        vt = jnp.concatenate([vt_ref[c * sub_k + j, va, :] for j in range(sub_k)], axis=1)
        acc = alpha * acc_refs[g][...] + jnp.dot(vt, p_refs[g][...], preferred_element_type=F32)
        acc_refs[g][...] = acc
        out = (acc[:V_HEAD] / acc[V_HEAD:V_HEAD + 1]).astype(BF16)
        for j in range(sub_q):
            o_ref[qi * sub_q + j, g * V_HEAD:(g + 1) * V_HEAD, :] = out[:, j * tm:(j + 1) * tm]

    carry = []
    for g in range(n_streams):
        acc_refs[g][...] = jnp.zeros_like(acc_refs[g])
        mc = scores(g, 0)
        m, alpha = probs(g, 0, jnp.zeros((1, tq), F32), mc)
        carry.append((m, scores(g, 1), alpha))

    def step(t, carry):
        out = []
        for g, (m, mc, alpha_prev) in enumerate(carry):
            weighted_values(g, t - 1, alpha_prev)
            m, alpha = probs(g, t, m, mc)
            out.append((m, scores(g, t + 1), alpha))
        return tuple(out)

    carry = lax.fori_loop(1, n_pairs - 1, step, tuple(carry))
    for g, (m, mc, alpha_prev) in enumerate(carry):
        weighted_values(g, n_pairs - 2, alpha_prev)
        m, alpha = probs(g, n_pairs - 1, m, mc)
        weighted_values(g, n_pairs - 1, alpha)


def _attn_call(qt, k, vt, *, batch, seq, tq, tk, n_streams):
    n_tiles, _, tm = qt.shape
    tiles_per_seq = n_tiles // batch
    kern = functools.partial(_attn_kernel, tk=tk, tq=tq, tm=tm, n_chunks=seq // tk, n_qtiles=seq // tq,
                             n_streams=n_streams)
    return pl.pallas_call(
        kern,
        grid=(batch, MLA_HEADS // n_streams),
        in_specs=[
            pl.BlockSpec((tiles_per_seq, n_streams * HEAD_PAD, tm), lambda b, h: (b, h, 0)),
            pl.BlockSpec((seq, n_streams * HEAD_PAD), lambda b, h: (b, h)),
            pl.BlockSpec((tiles_per_seq, n_streams * V_AUG, tm), lambda b, h: (b, h, 0)),
        ],
        out_specs=pl.BlockSpec((tiles_per_seq, n_streams * V_HEAD, tm), lambda b, h: (b, h, 0)),
        out_shape=jax.ShapeDtypeStruct((n_tiles, MLA_WIDTH, tm), BF16),
        scratch_shapes=([pltpu.VMEM((tk, tq), F32)] * n_streams + [pltpu.VMEM((tk, tq), BF16)] * n_streams
                        + [pltpu.VMEM((V_AUG, tq), F32)] * n_streams),
        compiler_params=pltpu.CompilerParams(dimension_semantics=("arbitrary", "arbitrary"),
                                             vmem_limit_bytes=VMEM_LIMIT_BYTES),
        name="attn",
    )(qt, k, vt)


def _oproj_kernel(x_ref, yc_ref, ymt_ref, woc_ref, wom_ref, gffn_ref, x1_ref, hf_ref):
    x1 = (x_ref[...]
          + jnp.dot(yc_ref[...], woc_ref[...], preferred_element_type=F32)
          + lax.dot_general(ymt_ref[0], wom_ref[...], _TN, preferred_element_type=F32))
    x1_ref[...] = x1
    hf_ref[...] = _rms(x1, gffn_ref[...]).astype(BF16)


def _oproj_call(x2d, yc, ymt, woc, wom, gffn, *, tm):
    t = x2d.shape[0]
    row_spec = lambda w: pl.BlockSpec((tm, w), lambda i: (i, 0))
    return pl.pallas_call(
        _oproj_kernel,
        grid=(t // tm,),
        in_specs=[row_spec(D_MODEL), row_spec(CONV_WIDTH),
                  pl.BlockSpec((1, MLA_WIDTH, tm), lambda i: (i, 0, 0)),
                  _resident(woc.shape), _resident(wom.shape), _resident(gffn.shape)],
        out_specs=[row_spec(D_MODEL), row_spec(D_MODEL)],
        out_shape=[jax.ShapeDtypeStruct((t, D_MODEL), F32),
                   jax.ShapeDtypeStruct((t, D_MODEL), BF16)],
        compiler_params=pltpu.CompilerParams(dimension_semantics=("arbitrary",),
                                             vmem_limit_bytes=VMEM_LIMIT_BYTES),
        name="oproj",
    )(x2d, yc, ymt, woc, wom, gffn)


def _ffn_kernel(hp_ref, hm_ref, hn_ref, x1_ref, p_ref, wup_ref, cw_ref, cb_ref, wdown_ref,
                gple_ref, wgate_ref, wproj_ref, gfin_ref, o_ref, a0_ref, a1_ref, act_ref, *, tm, tiles_per_seq):
    i = pl.program_id(0)
    halo = BF16_SUBLANES
    n_ext = tm + 2 * halo

    first = (i % tiles_per_seq) == 0
    last = (i % tiles_per_seq) == tiles_per_seq - 1
    zero = jnp.zeros((halo, D_MODEL), BF16)
    h_ext = jnp.concatenate([jnp.where(first, zero, hp_ref[...]), hm_ref[...],
                             jnp.where(last, zero, hn_ref[...])], axis=0)

    a_bufs = (a0_ref, a1_ref)

    def up_proj(c, slot):
        a_bufs[slot][:, :FF_CHUNK] = jnp.dot(h_ext, wup_ref[c], preferred_element_type=F32)
        a_bufs[slot][:, FF_CHUNK:] = jnp.dot(h_ext, wup_ref[c + N_FF_CHUNKS], preferred_element_type=F32)

    def conv_half(a, c):
        w = cw_ref[c]
        a_prev = pltpu.roll(a, 1, 0)[halo:halo + tm]
        a_next = pltpu.roll(a, n_ext - 1, 0)[halo:halo + tm]
        return a_prev * w[0:1] + a[halo:halo + tm] * w[1:2] + a_next * w[2:3] + cb_ref[c]

    def gated_act(c, slot):
        g = conv_half(a_bufs[slot][:, :FF_CHUNK], c)
        u = conv_half(a_bufs[slot][:, FF_CHUNK:], c + N_FF_CHUNKS)
        act_ref[c] = (g * jax.nn.sigmoid(g) * u).astype(BF16)

    up_proj(0, 0)

    def body(i, carry):
        c = 2 * i
        up_proj(c + 1, 1)
        gated_act(c, 0)
        up_proj(c + 2, 0)
        gated_act(c + 1, 1)
        return carry

    assert N_FF_CHUNKS % 2 == 1
    lax.fori_loop(0, N_FF_CHUNKS // 2, body, 0)
    gated_act(N_FF_CHUNKS - 1, 0)

    act = jnp.concatenate([act_ref[c] for c in range(N_FF_CHUNKS)], axis=1)
    x2 = x1_ref[...] + jnp.dot(act, wdown_ref[...], preferred_element_type=F32)
    gate = jax.nn.sigmoid(jnp.dot(_rms(x2, gple_ref[...]).astype(BF16), wgate_ref[...],
                                  preferred_element_type=F32))
    pp = jnp.dot(p_ref[...].astype(BF16), wproj_ref[...], preferred_element_type=F32)
    x3 = x2 + gate * pp
    o_ref[...] = _rms(x3, gfin_ref[...])


def _ffn_call(hf, x1, p2d, wup, cw, cb, wdown, gple, wgate, wproj, gfin, *, seq, tm):
    t = hf.shape[0]
    tiles_per_seq = seq // tm
    hb = tm // BF16_SUBLANES
    n_hblocks = t // BF16_SUBLANES
    kern = functools.partial(_ffn_kernel, tm=tm, tiles_per_seq=tiles_per_seq)
    row_spec = lambda w: pl.BlockSpec((tm, w), lambda i: (i, 0))
    return pl.pallas_call(
        kern,
        grid=(t // tm,),
        in_specs=[
            pl.BlockSpec((BF16_SUBLANES, D_MODEL), lambda i: (jnp.maximum(i * hb - 1, 0), 0)),
            row_spec(D_MODEL),
            pl.BlockSpec((BF16_SUBLANES, D_MODEL), lambda i: (jnp.minimum((i + 1) * hb, n_hblocks - 1), 0)),
            row_spec(D_MODEL), row_spec(PLE_DIM),
            _resident(wup.shape), _resident(cw.shape), _resident(cb.shape), _resident(wdown.shape),
            _resident(gple.shape), _resident(wgate.shape), _resident(wproj.shape), _resident(gfin.shape),
        ],
        out_specs=row_spec(D_MODEL),
        out_shape=jax.ShapeDtypeStruct((t, D_MODEL), F32),
        scratch_shapes=[pltpu.VMEM((tm + 2 * BF16_SUBLANES, 2 * FF_CHUNK), F32),
                        pltpu.VMEM((tm + 2 * BF16_SUBLANES, 2 * FF_CHUNK), F32),
                        pltpu.VMEM((N_FF_CHUNKS, tm, FF_CHUNK), BF16)],
        compiler_params=pltpu.CompilerParams(dimension_semantics=("arbitrary",),
                                             vmem_limit_bytes=VMEM_LIMIT_BYTES),
        name="ffn",
    )(hf, hf, hf, x1, p2d, wup, cw, cb, wdown, gple, wgate, wproj, gfin)


def _swap_halves(w):
    half = w.shape[-1] // 2
    return jnp.concatenate([w[..., half:], w[..., :half]], axis=-1)


def _rope_tables(seq):
    pos = jnp.arange(seq, dtype=F32)
    inv_freq = ROPE_THETA ** (-jnp.arange(0, QK_ROPE, 2, dtype=F32) / QK_ROPE)
    ang = pos[:, None] * inv_freq[None, :]
    cos = jnp.cos(jnp.concatenate([ang, ang], axis=-1))
    sin = jnp.sin(jnp.concatenate([-ang, ang], axis=-1))
    pad = lambda a: jnp.pad(a, ((0, 0), (QK_NOPE, HEAD_PAD - QK_NOPE - QK_ROPE)))
    return pad(cos), pad(sin)


def _layout_weights(w_in, w_uq, w_ukv):
    d_conv_q_kv = 3 * CONV_WIDTH + Q_LORA + KV_LORA
    w_kr = w_in[:, d_conv_q_kv:]
    lane_pad = ((0, 0), (QK_NOPE, HEAD_PAD - QK_NOPE - QK_ROPE))
    win = jnp.concatenate([w_in[:, :d_conv_q_kv], jnp.pad(w_kr, lane_pad),
                           jnp.pad(_swap_halves(w_kr), lane_pad)], axis=1)

    wq3 = w_uq.reshape(Q_LORA, MLA_HEADS, QK_NOPE + QK_ROPE)
    nope, rope = wq3[..., :QK_NOPE], wq3[..., QK_NOPE:]
    zpad = jnp.zeros((Q_LORA, MLA_HEADS, HEAD_PAD - QK_NOPE - QK_ROPE), w_uq.dtype)
    wq_a = jnp.concatenate([nope, rope, zpad], axis=-1).reshape(Q_LORA, MLA_HEADS * HEAD_PAD)
    wq_b = jnp.concatenate([jnp.zeros_like(nope), _swap_halves(rope), zpad], axis=-1)
    wq = jnp.concatenate([wq_a, wq_b.reshape(Q_LORA, MLA_HEADS * HEAD_PAD)], axis=1)

    wkv3 = w_ukv.reshape(KV_LORA, MLA_HEADS, QK_NOPE + V_HEAD)
    k_nope, v = wkv3[..., :QK_NOPE], wkv3[..., QK_NOPE:]
    wk = jnp.pad(k_nope, ((0, 0), (0, 0), (0, HEAD_PAD - QK_NOPE))).reshape(KV_LORA, MLA_HEADS * HEAD_PAD)
    wv = v.reshape(KV_LORA, MLA_WIDTH)
    return win.astype(BF16), wq.T.astype(BF16), wk.astype(BF16), wv.T.astype(BF16)


def _chunk_cols(w):
    r, c = w.shape
    return w.reshape(r, c // FF_CHUNK, FF_CHUNK).transpose(1, 0, 2)


def kernel(x, p, norm_mix_g, w_in, conv_w, q_norm_g, w_uq, kv_norm_g, w_ukv, w_o, norm_ffn_g, w_up, ffn_conv_w, ffn_conv_b, w_down, ple_norm_g, w_ple_gate, w_ple_proj, final_norm_g):
    batch, seq, d = x.shape
    depth = w_in.shape[0]
    t = batch * seq
    tm = 512
    assert seq % tm == 0 and d == D_MODEL and depth == 1
    cos_k, sin_k = _rope_tables(seq)
    row = lambda g: g.reshape(1, -1)

    x2d = x.reshape(t, d)
    for i in range(depth):
        win, wqt, wk, wvt = _layout_weights(w_in[i], w_uq[i], w_ukv[i])
        yc, qt, k, vt = _proj_call(x2d, row(norm_mix_g[i]), win, conv_w[i], row(q_norm_g[i]), wqt,
                                   row(kv_norm_g[i]), wk, wvt, cos_k, sin_k, cos_k.T, sin_k.T,
                                   seq=seq, tm=tm)
        ymt = _attn_call(qt, k, vt, batch=batch, seq=seq, tq=1024, tk=512, n_streams=2)
        wo = w_o[i].astype(BF16)
        x1, hf = _oproj_call(x2d, yc, ymt, wo[:CONV_WIDTH], wo[CONV_WIDTH:], row(norm_ffn_g[i]), tm=tm)
        x2d = _ffn_call(hf, x1, p[i].reshape(t, PLE_DIM),
                        _chunk_cols(w_up[i].astype(BF16)), _chunk_cols(ffn_conv_w[i]),
                        _chunk_cols(ffn_conv_b[i].reshape(1, -1)),
                        w_down[i].astype(BF16),
                        row(ple_norm_g[i]), w_ple_gate[i].astype(BF16), w_ple_proj[i].astype(BF16),
                        row(final_norm_g), seq=seq, tm=tm)
    return x2d.reshape(batch, seq, d)
```

```python
import functools
import math

import jax
import jax.numpy as jnp
from jax import lax
from jax.experimental import pallas as pl
from jax.experimental.pallas import tpu as pltpu

D_MODEL = 1024
PLE_DIM = 256
CONV_WIDTH = 512
MLA_HEADS = 8
Q_LORA = 256
KV_LORA = 128
QK_NOPE = 64
QK_ROPE = 32
V_HEAD = 64
MLA_WIDTH = MLA_HEADS * V_HEAD
D_FF = 2816
ROPE_THETA = 10000.0
EPS = 1e-6

HEAD_PAD = 128
FF_CHUNK = 256
N_FF_CHUNKS = D_FF // FF_CHUNK
F32_SUBLANES = 8
BF16_SUBLANES = 16
V_AUG = V_HEAD + BF16_SUBLANES
VMEM_LIMIT_BYTES = 56 * 1024 * 1024

BF16 = jnp.bfloat16
F32 = jnp.float32
_NT = (((1,), (1,)), ((), ()))
_TN = (((0,), (0,)), ((), ()))


def _silu(x):
    h = 0.5 * x
    return h + h * jnp.tanh(h)


def _rms(x, g):
    return x * lax.rsqrt(jnp.mean(x * x, axis=-1, keepdims=True) + EPS) * g


def _conv3_rows(a, w, b, lo, n):
    rows, cols = a.shape
    sl = slice(lo // F32_SUBLANES, (lo + n) // F32_SUBLANES)
    a3 = a.reshape(rows // F32_SUBLANES, F32_SUBLANES, cols)
    sub = lax.broadcasted_iota(jnp.int32, (1, F32_SUBLANES, cols), 1)
    down = pltpu.roll(a3, 1, 1)
    up = pltpu.roll(a3, F32_SUBLANES - 1, 1)
    prev = jnp.where(sub == 0, jnp.concatenate([down[-1:], down[:-1]], axis=0), down)
    nxt = jnp.where(sub == F32_SUBLANES - 1, jnp.concatenate([up[1:], up[:1]], axis=0), up)
    tile = lambda r: jnp.broadcast_to(r, (F32_SUBLANES, cols))[None]
    out = prev[sl] * tile(w[0:1]) + a3[sl] * tile(w[1:2]) + nxt[sl] * tile(w[2:3]) + tile(b)
    return out.reshape(n, cols)


def _resident(shape):
    zeros = (0,) * len(shape)
    return pl.BlockSpec(shape, lambda *_: zeros, pipeline_mode=pl.Buffered(1))


def _proj_kernel(xp_ref, xm_ref, xn_ref, gmix_ref, win_ref, convw_ref, gq_ref, wqt_ref,
                 gkv_ref, wk_ref, wvt_ref, cos_ref, sin_ref, cost_ref, sint_ref,
                 yconv_ref, qt_ref, k_ref, vt_ref, *, tm, tiles_per_seq, q_scale):
    i = pl.program_id(0)
    halo = F32_SUBLANES
    x_ext = jnp.concatenate([xp_ref[...], xm_ref[...], xn_ref[...]], axis=0)
    h = _rms(x_ext, gmix_ref[...]).astype(BF16)
    z = jnp.dot(h, win_ref[...], preferred_element_type=F32)

    first = (i % tiles_per_seq) == 0
    last = (i % tiles_per_seq) == tiles_per_seq - 1
    lo = jnp.where(first, halo, 0)
    hi = jnp.where(last, tm + halo, tm + 2 * halo)
    row = lax.broadcasted_iota(jnp.int32, (tm + 2 * halo, 1), 0)
    u = z[:, 2 * CONV_WIDTH:3 * CONV_WIDTH] * z[:, 0:CONV_WIDTH]
    u = jnp.where((row >= lo) & (row < hi), u, 0.0)
    n_ext = tm + 2 * halo
    u_prev = pltpu.roll(u, 1, 0)[halo:halo + tm]
    u_next = pltpu.roll(u, n_ext - 1, 0)[halo:halo + tm]
    cw = convw_ref[...]
    conv = u_prev * cw[0:1] + u[halo:halo + tm] * cw[1:2] + u_next * cw[2:3]
    zm = z[halo:halo + tm]
    yconv_ref[...] = (zm[:, CONV_WIDTH:2 * CONV_WIDTH] * conv).astype(BF16)

    o = 3 * CONV_WIDTH
    q_lat = zm[:, o:o + Q_LORA]
    kv_lat = zm[:, o + Q_LORA:o + Q_LORA + KV_LORA]
    o2 = o + Q_LORA + KV_LORA
    kr = zm[:, o2:o2 + HEAD_PAD]
    kr_sw = zm[:, o2 + HEAD_PAD:o2 + 2 * HEAD_PAD]

    qn = _rms(q_lat, gq_ref[...]).astype(BF16)
    qab_t = lax.dot_general(wqt_ref[...], qn, _NT, preferred_element_type=F32)
    sub = lax.broadcasted_iota(jnp.int32, (HEAD_PAD, 1), 0)
    cos_qt = (cost_ref[...] + jnp.where(sub < QK_NOPE, 1.0, 0.0)) * q_scale
    sin_qt = sint_ref[...] * q_scale
    hw = MLA_HEADS * HEAD_PAD
    for hd in range(MLA_HEADS):
        sl = slice(hd * HEAD_PAD, (hd + 1) * HEAD_PAD)
        sl2 = slice(hw + hd * HEAD_PAD, hw + (hd + 1) * HEAD_PAD)
        qt_ref[0, sl, :] = (qab_t[sl] * cos_qt + qab_t[sl2] * sin_qt).astype(BF16)

    kvn = _rms(kv_lat, gkv_ref[...]).astype(BF16)
    ko = jnp.dot(kvn, wk_ref[...], preferred_element_type=F32)
    k_rope = kr * cos_ref[...] + kr_sw * sin_ref[...]
    for hd in range(MLA_HEADS):
        sl = slice(hd * HEAD_PAD, (hd + 1) * HEAD_PAD)
        k_ref[:, sl] = (ko[:, sl] + k_rope).astype(BF16)
    vt = lax.dot_general(wvt_ref[...], kvn, _NT, preferred_element_type=F32).astype(BF16)
    ones_rows = (lax.broadcasted_iota(jnp.int32, (V_AUG - V_HEAD, tm), 0) == 0).astype(BF16)
    for hd in range(MLA_HEADS):
        vt_ref[0, hd * V_AUG:hd * V_AUG + V_HEAD, :] = vt[hd * V_HEAD:(hd + 1) * V_HEAD]
        vt_ref[0, hd * V_AUG + V_HEAD:(hd + 1) * V_AUG, :] = ones_rows


def _proj_call(x2d, gmix, win, convw, gq, wqt, gkv, wk, wvt, cos_k, sin_k, cos_t, sin_t, *, seq, tm):
    t = x2d.shape[0]
    n_tiles = t // tm
    tiles_per_seq = seq // tm
    hb = tm // F32_SUBLANES
    n_hblocks = t // F32_SUBLANES
    hw = MLA_HEADS * HEAD_PAD
    q_scale = (QK_NOPE + QK_ROPE) ** -0.5 * math.log2(math.e)
    kern = functools.partial(_proj_kernel, tm=tm, tiles_per_seq=tiles_per_seq, q_scale=q_scale)
    row_spec = lambda w: pl.BlockSpec((tm, w), lambda i: (i, 0))
    return pl.pallas_call(
        kern,
        grid=(n_tiles,),
        in_specs=[
            pl.BlockSpec((F32_SUBLANES, D_MODEL), lambda i: (jnp.maximum(i * hb - 1, 0), 0)),
            row_spec(D_MODEL),
            pl.BlockSpec((F32_SUBLANES, D_MODEL), lambda i: (jnp.minimum((i + 1) * hb, n_hblocks - 1), 0)),
            _resident(gmix.shape), _resident(win.shape), _resident(convw.shape),
            _resident(gq.shape), _resident(wqt.shape), _resident(gkv.shape),
            _resident(wk.shape), _resident(wvt.shape),
            pl.BlockSpec((tm, HEAD_PAD), lambda i: (i % tiles_per_seq, 0)),
            pl.BlockSpec((tm, HEAD_PAD), lambda i: (i % tiles_per_seq, 0)),
            pl.BlockSpec((HEAD_PAD, tm), lambda i: (0, i % tiles_per_seq)),
            pl.BlockSpec((HEAD_PAD, tm), lambda i: (0, i % tiles_per_seq)),
        ],
        out_specs=[row_spec(CONV_WIDTH),
                   pl.BlockSpec((1, hw, tm), lambda i: (i, 0, 0)),
                   row_spec(hw),
                   pl.BlockSpec((1, MLA_HEADS * V_AUG, tm), lambda i: (i, 0, 0))],
        out_shape=[jax.ShapeDtypeStruct((t, CONV_WIDTH), BF16),
                   jax.ShapeDtypeStruct((n_tiles, hw, tm), BF16),
                   jax.ShapeDtypeStruct((t, hw), BF16),
                   jax.ShapeDtypeStruct((n_tiles, MLA_HEADS * V_AUG, tm), BF16)],
        compiler_params=pltpu.CompilerParams(dimension_semantics=("arbitrary",),
                                             vmem_limit_bytes=VMEM_LIMIT_BYTES),
        name="proj",
    )(x2d, x2d, x2d, gmix, win, convw, gq, wqt, gkv, wk, wvt, cos_k, sin_k, cos_t, sin_t)


def _attn_kernel(qt_ref, k_ref, vt_ref, o_ref, *scratch, tk, tq, tm, cb, n_chunks, n_qtiles, n_streams):
    s_refs = scratch[0:n_streams]
    p_refs = scratch[n_streams:2 * n_streams]
    acc_refs = scratch[2 * n_streams:3 * n_streams]
    sub_q = tq // tm
    sub_k = tk // tm
    shift = n_chunks.bit_length() - 1
    assert 1 << shift == n_chunks
    n_pairs = n_qtiles * n_chunks

    def col_pieces(n):
        w = min(cb, tm)
        return [((n * cb + o) // tm, slice((n * cb + o) % tm, (n * cb + o) % tm + w), slice(o, o + w))
                for o in range(0, cb, w)]

    def stream_step(g, t, m, mc, alpha_prev, *, pv=True, ex=True, qk=True):
        hd = slice(g * HEAD_PAD, (g + 1) * HEAD_PAD)
        if pv:
            c0 = (t - 1) & (n_chunks - 1)
            qi0 = (t - 1) >> shift
            va = slice(g * V_AUG, (g + 1) * V_AUG)
            vt = jnp.concatenate([vt_ref[c0 * sub_k + j, va, :] for j in range(sub_k)], axis=1)
        if ex:
            m = jnp.where((t & (n_chunks - 1)) == 0, -jnp.inf, m)
            m_new = jnp.maximum(m, mc)
            alpha = jnp.exp2(m - m_new)
        else:
            m_new, alpha = m, None
        if qk:
            c1 = (t + 1) & (n_chunks - 1)
            qi1 = (t + 1) >> shift
            off = pl.multiple_of(c1 * tk, tk)
            k = k_ref[pl.ds(off, tk), hd]
        mc_parts = []
        for n in range(tq // cb):
            cols = slice(n * cb, (n + 1) * cb)
            if pv:
                acc = alpha_prev[:, cols] * acc_refs[g][:, cols] + jnp.dot(
                    vt, p_refs[g][:, cols], preferred_element_type=F32)
                acc_refs[g][:, cols] = acc
                out = (acc[:V_HEAD] / acc[V_HEAD:V_HEAD + 1]).astype(BF16)
                for tile, lanes, part in col_pieces(n):
                    o_ref[qi0 * sub_q + tile, g * V_HEAD:(g + 1) * V_HEAD, lanes] = out[:, part]
            if ex:
                p_refs[g][:, cols] = jnp.exp2(s_refs[g][:, cols] - m_new[:, cols]).astype(BF16)
            if qk:
                qt = jnp.concatenate([qt_ref[qi1 * sub_q + tile, hd, lanes]
                                      for tile, lanes, _ in col_pieces(n)], axis=1)
                sc = jnp.dot(k, qt, preferred_element_type=F32)
                s_refs[g][:, cols] = sc
                mc_parts.append(jnp.max(sc, axis=0, keepdims=True))
        mc_next = jnp.concatenate(mc_parts, axis=1) if qk else None
        return m_new, mc_next, alpha

    carry = []
    for g in range(n_streams):
        acc_refs[g][...] = jnp.zeros_like(acc_refs[g])
        m0 = jnp.zeros((1, tq), F32)
        _, mc, _ = stream_step(g, -1, m0, None, None, pv=False, ex=False)
        carry.append(stream_step(g, 0, m0, mc, None, pv=False))

    def step(t, carry):
        return tuple(stream_step(g, t, *c) for g, c in enumerate(carry))

    carry = lax.fori_loop(1, n_pairs - 1, step, tuple(carry))
    for g, (m, mc, alpha_prev) in enumerate(carry):
        m, _, alpha = stream_step(g, n_pairs - 1, m, mc, alpha_prev, qk=False)
        stream_step(g, n_pairs, m, None, alpha, ex=False, qk=False)


def _attn_call(qt, k, vt, *, batch, seq, tq, tk, cb, n_streams):
    n_tiles, _, tm = qt.shape
    tiles_per_seq = n_tiles // batch
    kern = functools.partial(_attn_kernel, tk=tk, tq=tq, tm=tm, cb=cb, n_chunks=seq // tk, n_qtiles=seq // tq,
                             n_streams=n_streams)
    return pl.pallas_call(
        kern,
        grid=(batch, MLA_HEADS // n_streams),
        in_specs=[
            pl.BlockSpec((tiles_per_seq, n_streams * HEAD_PAD, tm), lambda b, h: (b, h, 0)),
            pl.BlockSpec((seq, n_streams * HEAD_PAD), lambda b, h: (b, h)),
            pl.BlockSpec((tiles_per_seq, n_streams * V_AUG, tm), lambda b, h: (b, h, 0)),
        ],
        out_specs=pl.BlockSpec((tiles_per_seq, n_streams * V_HEAD, tm), lambda b, h: (b, h, 0)),
        out_shape=jax.ShapeDtypeStruct((n_tiles, MLA_WIDTH, tm), BF16),
        scratch_shapes=([pltpu.VMEM((tk, tq), F32)] * n_streams + [pltpu.VMEM((tk, tq), BF16)] * n_streams
                        + [pltpu.VMEM((V_AUG, tq), F32)] * n_streams),
        compiler_params=pltpu.CompilerParams(dimension_semantics=("arbitrary", "arbitrary"),
                                             vmem_limit_bytes=VMEM_LIMIT_BYTES),
        name="attn",
    )(qt, k, vt)


def _oproj_kernel(x_ref, yc_ref, ymt_ref, woc_ref, wom_ref, gffn_ref, x1_ref, hf_ref):
    x1 = (x_ref[...]
          + jnp.dot(yc_ref[...], woc_ref[...], preferred_element_type=F32)
          + lax.dot_general(ymt_ref[0], wom_ref[...], _TN, preferred_element_type=F32))
    x1_ref[...] = x1
    hf_ref[...] = _rms(x1, gffn_ref[...]).astype(BF16)


def _oproj_call(x2d, yc, ymt, woc, wom, gffn, *, tm):
    t = x2d.shape[0]
    row_spec = lambda w: pl.BlockSpec((tm, w), lambda i: (i, 0))
    return pl.pallas_call(
        _oproj_kernel,
        grid=(t // tm,),
        in_specs=[row_spec(D_MODEL), row_spec(CONV_WIDTH),
                  pl.BlockSpec((1, MLA_WIDTH, tm), lambda i: (i, 0, 0)),
                  _resident(woc.shape), _resident(wom.shape), _resident(gffn.shape)],
        out_specs=[row_spec(D_MODEL), row_spec(D_MODEL)],
        out_shape=[jax.ShapeDtypeStruct((t, D_MODEL), F32),
                   jax.ShapeDtypeStruct((t, D_MODEL), BF16)],
        compiler_params=pltpu.CompilerParams(dimension_semantics=("arbitrary",),
                                             vmem_limit_bytes=VMEM_LIMIT_BYTES),
        name="oproj",
    )(x2d, yc, ymt, woc, wom, gffn)


def _ffn_kernel(hp_ref, hm_ref, hn_ref, x1_ref, p_ref, wup_ref, cw_ref, cb_ref, wdown_ref,
                gple_ref, wgate_ref, wproj_ref, gfin_ref, o_ref, a0_ref, a1_ref, act_ref, *, tm, tiles_per_seq):
    i = pl.program_id(0)
    halo = BF16_SUBLANES
    n_ext = tm + 2 * halo

    first = (i % tiles_per_seq) == 0
    last = (i % tiles_per_seq) == tiles_per_seq - 1
    zero = jnp.zeros((halo, D_MODEL), BF16)
    h_ext = jnp.concatenate([jnp.where(first, zero, hp_ref[...]), hm_ref[...],
                             jnp.where(last, zero, hn_ref[...])], axis=0)

    a_bufs = (a0_ref, a1_ref)

    def up_proj(c, slot):
        a_bufs[slot][:, :FF_CHUNK] = jnp.dot(h_ext, wup_ref[c], preferred_element_type=F32)
        a_bufs[slot][:, FF_CHUNK:] = jnp.dot(h_ext, wup_ref[c + N_FF_CHUNKS], preferred_element_type=F32)

    def conv_half(a, c):
        return _conv3_rows(a, cw_ref[c], cb_ref[c], halo, tm)

    def gated_act(c, slot):
        g = conv_half(a_bufs[slot][:, :FF_CHUNK], c)
        u = conv_half(a_bufs[slot][:, FF_CHUNK:], c + N_FF_CHUNKS)
        act_ref[c] = (_silu(g) * u).astype(BF16)

    up_proj(0, 0)

    def body(i, carry):
        c = 2 * i
        up_proj(c + 1, 1)
        gated_act(c, 0)
        up_proj(c + 2, 0)
        gated_act(c + 1, 1)
        return carry

    assert N_FF_CHUNKS % 2 == 1
    lax.fori_loop(0, N_FF_CHUNKS // 2, body, 0)
    gated_act(N_FF_CHUNKS - 1, 0)

    act = jnp.concatenate([act_ref[c] for c in range(N_FF_CHUNKS)], axis=1)
    x2 = x1_ref[...] + jnp.dot(act, wdown_ref[...], preferred_element_type=F32)
    gate = jax.nn.sigmoid(jnp.dot(_rms(x2, gple_ref[...]).astype(BF16), wgate_ref[...],
                                  preferred_element_type=F32))
    pp = jnp.dot(p_ref[...].astype(BF16), wproj_ref[...], preferred_element_type=F32)
    x3 = x2 + gate * pp
    o_ref[...] = _rms(x3, gfin_ref[...])


def _ffn_call(hf, x1, p2d, wup, cw, cb, wdown, gple, wgate, wproj, gfin, *, seq, tm):
    t = hf.shape[0]
    tiles_per_seq = seq // tm
    hb = tm // BF16_SUBLANES
    n_hblocks = t // BF16_SUBLANES
    kern = functools.partial(_ffn_kernel, tm=tm, tiles_per_seq=tiles_per_seq)
    row_spec = lambda w: pl.BlockSpec((tm, w), lambda i: (i, 0))
    return pl.pallas_call(
        kern,
        grid=(t // tm,),
        in_specs=[
            pl.BlockSpec((BF16_SUBLANES, D_MODEL), lambda i: (jnp.maximum(i * hb - 1, 0), 0)),
            row_spec(D_MODEL),
            pl.BlockSpec((BF16_SUBLANES, D_MODEL), lambda i: (jnp.minimum((i + 1) * hb, n_hblocks - 1), 0)),
            row_spec(D_MODEL), row_spec(PLE_DIM),
            _resident(wup.shape), _resident(cw.shape), _resident(cb.shape), _resident(wdown.shape),
            _resident(gple.shape), _resident(wgate.shape), _resident(wproj.shape), _resident(gfin.shape),
        ],
        out_specs=row_spec(D_MODEL),
        out_shape=jax.ShapeDtypeStruct((t, D_MODEL), F32),
        scratch_shapes=[pltpu.VMEM((tm + 2 * BF16_SUBLANES, 2 * FF_CHUNK), F32),
                        pltpu.VMEM((tm + 2 * BF16_SUBLANES, 2 * FF_CHUNK), F32),
                        pltpu.VMEM((N_FF_CHUNKS, tm, FF_CHUNK), BF16)],
        compiler_params=pltpu.CompilerParams(dimension_semantics=("arbitrary",),
                                             vmem_limit_bytes=VMEM_LIMIT_BYTES),
        name="ffn",
    )(hf, hf, hf, x1, p2d, wup, cw, cb, wdown, gple, wgate, wproj, gfin)


def _swap_halves(w):
    half = w.shape[-1] // 2
    return jnp.concatenate([w[..., half:], w[..., :half]], axis=-1)


def _rope_tables(seq):
    pos = jnp.arange(seq, dtype=F32)
    inv_freq = ROPE_THETA ** (-jnp.arange(0, QK_ROPE, 2, dtype=F32) / QK_ROPE)
    ang = pos[:, None] * inv_freq[None, :]
    cos = jnp.cos(jnp.concatenate([ang, ang], axis=-1))
    sin = jnp.sin(jnp.concatenate([-ang, ang], axis=-1))
    pad = lambda a: jnp.pad(a, ((0, 0), (QK_NOPE, HEAD_PAD - QK_NOPE - QK_ROPE)))
    return pad(cos), pad(sin)


def _layout_weights(w_in, w_uq, w_ukv):
    d_conv_q_kv = 3 * CONV_WIDTH + Q_LORA + KV_LORA
    w_kr = w_in[:, d_conv_q_kv:]
    lane_pad = ((0, 0), (QK_NOPE, HEAD_PAD - QK_NOPE - QK_ROPE))
    win = jnp.concatenate([w_in[:, :d_conv_q_kv], jnp.pad(w_kr, lane_pad),
                           jnp.pad(_swap_halves(w_kr), lane_pad)], axis=1)

    wq3 = w_uq.reshape(Q_LORA, MLA_HEADS, QK_NOPE + QK_ROPE)
    nope, rope = wq3[..., :QK_NOPE], wq3[..., QK_NOPE:]
    zpad = jnp.zeros((Q_LORA, MLA_HEADS, HEAD_PAD - QK_NOPE - QK_ROPE), w_uq.dtype)
    wq_a = jnp.concatenate([nope, rope, zpad], axis=-1).reshape(Q_LORA, MLA_HEADS * HEAD_PAD)
    wq_b = jnp.concatenate([jnp.zeros_like(nope), _swap_halves(rope), zpad], axis=-1)
    wq = jnp.concatenate([wq_a, wq_b.reshape(Q_LORA, MLA_HEADS * HEAD_PAD)], axis=1)

    wkv3 = w_ukv.reshape(KV_LORA, MLA_HEADS, QK_NOPE + V_HEAD)
    k_nope, v = wkv3[..., :QK_NOPE], wkv3[..., QK_NOPE:]
    wk = jnp.pad(k_nope, ((0, 0), (0, 0), (0, HEAD_PAD - QK_NOPE))).reshape(KV_LORA, MLA_HEADS * HEAD_PAD)
    wv = v.reshape(KV_LORA, MLA_WIDTH)
    return win.astype(BF16), wq.T.astype(BF16), wk.astype(BF16), wv.T.astype(BF16)


def _chunk_cols(w):
    r, c = w.shape
    return w.reshape(r, c // FF_CHUNK, FF_CHUNK).transpose(1, 0, 2)


def kernel(x, p, norm_mix_g, w_in, conv_w, q_norm_g, w_uq, kv_norm_g, w_ukv, w_o, norm_ffn_g, w_up, ffn_conv_w, ffn_conv_b, w_down, ple_norm_g, w_ple_gate, w_ple_proj, final_norm_g):
    batch, seq, d = x.shape
    depth = w_in.shape[0]
    t = batch * seq
    tm = 512
    assert seq % tm == 0 and d == D_MODEL and depth == 1
    cos_k, sin_k = _rope_tables(seq)
    row = lambda g: g.reshape(1, -1)

    x2d = x.reshape(t, d)
    for i in range(depth):
        win, wqt, wk, wvt = _layout_weights(w_in[i], w_uq[i], w_ukv[i])
        yc, qt, k, vt = _proj_call(x2d, row(norm_mix_g[i]), win, conv_w[i], row(q_norm_g[i]), wqt,
                                   row(kv_norm_g[i]), wk, wvt, cos_k, sin_k, cos_k.T, sin_k.T,
                                   seq=seq, tm=tm)
        ymt = _attn_call(qt, k, vt, batch=batch, seq=seq, tq=1024, tk=512, cb=512, n_streams=2)
        wo = w_o[i].astype(BF16)
        x1, hf = _oproj_call(x2d, yc, ymt, wo[:CONV_WIDTH], wo[CONV_WIDTH:], row(norm_ffn_g[i]), tm=tm)
        x2d = _ffn_call(hf, x1, p[i].reshape(t, PLE_DIM),
                        _chunk_cols(w_up[i].astype(BF16)), _chunk_cols(ffn_conv_w[i]),
                        _chunk_cols(ffn_conv_b[i].reshape(1, -1)),
                        w_down[i].astype(BF16),
                        row(ple_norm_g[i]), w_ple_gate[i].astype(BF16), w_ple_proj[i].astype(BF16),
                        row(final_norm_g), seq=seq, tm=tm)
    return x2d.reshape(batch, seq, d)
```
